```python
import jax, jax.numpy as jnp
from jax import lax
import numpy as np

D_MODEL = 2048
BATCH = 32
SEQ = 256
DEPTH = 2
DEC_BATCH = 8
DEC_SEQ = 1024
PAST_LEN = 512

GRID_W = 64
N_MIXERS = 2
N_RG = (DEPTH + 1) // 2
N_CM = DEPTH // 2
D_RNN = D_MODEL
RG_HEADS = 16
RG_BW = D_RNN // RG_HEADS
RG_CONV_W = 4
RG_C = 8.0
CHUNK = 128
CM_WIDTH = D_MODEL
CM_GROUPS = 8
CM_GW = CM_WIDTH // CM_GROUPS
D_FF = 5632
FFN_CONV_W = 3
EPS = 1e-6
LN_EPS = 1e-5

kernel_name = 'hybrid_rglru_chunkgmlp_diffusion_step'


def _rmsnorm(x, g):
    x32 = x.astype(jnp.float32)
    y = x32 * lax.rsqrt(jnp.mean(x32 * x32, axis=-1, keepdims=True) + EPS)
    return (y * g.astype(jnp.float32)).astype(x.dtype)


def _modulation(cond, w, b):
    m = jax.nn.silu(cond) @ w + b
    return jnp.split(m[..., None, :], 6, axis=-1)


def _dwconv(x, w, b, pad_l, pad_r):
    T = x.shape[1]
    xp = jnp.pad(x, ((0, 0), (pad_l, pad_r), (0, 0)))
    y = xp[:, 0:T] * w[0]
    for k in range(1, w.shape[0]):
        y = y + xp[:, k:k + T] * w[k]
    return y + b


def _grid_pos_embed(rows, dtype):
    nf = D_MODEL // 4
    omega = 1.0 / (10000.0 ** (jnp.arange(nf, dtype=jnp.float32) / nf))
    rr, cc = jnp.meshgrid(jnp.arange(rows, dtype=jnp.float32),
                          jnp.arange(GRID_W, dtype=jnp.float32), indexing='ij')
    ar = rr.reshape(-1, 1) * omega
    ac = cc.reshape(-1, 1) * omega
    emb = jnp.concatenate([jnp.sin(ar), jnp.cos(ar), jnp.sin(ac), jnp.cos(ac)], axis=-1)
    return emb.astype(dtype)


def _rglru_mixer(xn, h0, w_in, conv_w, conv_b, w_a, b_a, w_x, b_x, lam, w_out):
    B, T, _ = xn.shape
    gate_in, xb = jnp.split(xn @ w_in, 2, axis=-1)
    gate = jax.nn.gelu(gate_in, approximate=True)
    xb = _dwconv(xb, conv_w, conv_b, 2, 1)
    xh = xb.reshape(B, T, RG_HEADS, RG_BW)
    r = jax.nn.sigmoid(jnp.einsum('bthi,dhij->btdhj', xh, w_a).reshape(B, T, 2, D_RNN) + b_a)
    i = jax.nn.sigmoid(jnp.einsum('bthi,dhij->btdhj', xh, w_x).reshape(B, T, 2, D_RNN) + b_x)
    log_a = -RG_C * r.astype(jnp.float32) * jax.nn.softplus(-lam.astype(jnp.float32))
    a = jnp.exp(log_a)
    u = jnp.sqrt(-jnp.expm1(2.0 * log_a)) * (i * xb[:, :, None, :]).astype(jnp.float32)
    a = jnp.stack([a[:, :, 0], a[:, ::-1, 1]], axis=2)
    u = jnp.stack([u[:, :, 0], u[:, ::-1, 1]], axis=2)

    def step(h, au):
        a_t, u_t = au
        h = a_t * h + u_t
        return h, h

    h_final, hs = lax.scan(step, h0.astype(jnp.float32),
                           (jnp.moveaxis(a, 1, 0), jnp.moveaxis(u, 1, 0)))
    y = hs[:, :, 0] + hs[::-1, :, 1]
    y = jnp.moveaxis(y, 0, 1).astype(xn.dtype)
    return (y * gate) @ w_out, h_final


def _chunk_gmlp_mixer(xn, w_in, b_in, ln_g, ln_b, w_s, b_s, w_out):
    B, T, _ = xn.shape
    n_chunks = T // CHUNK
    u, v = jnp.split(jax.nn.gelu(xn @ w_in + b_in, approximate=True), 2, axis=-1)
    v32 = v.astype(jnp.float32)
    mu = jnp.mean(v32, axis=-1, keepdims=True)
    var = jnp.mean(jnp.square(v32 - mu), axis=-1, keepdims=True)
    v = ((v32 - mu) * lax.rsqrt(var + LN_EPS) * ln_g + ln_b).astype(xn.dtype)
    v = v.reshape(B, n_chunks, CHUNK, CM_GROUPS, CM_GW)
    s = jnp.einsum('gpq,bnqgc->bnpgc', w_s, v) + b_s.T[:, :, None]
    return (u * s.reshape(B, T, CM_WIDTH)) @ w_out


def _conv_ffn(xn, w_up, conv_w, conv_b, w_down):
    h = _dwconv(xn @ w_up, conv_w, conv_b, 1, 1)
    g, v = jnp.split(h, 2, axis=-1)
    return (jax.nn.silu(g) * v) @ w_down


def _trunk(x, cond, h0, norm1_g, norm2_g, w_ada, b_ada,
           rg_w_in, rg_conv_w, rg_conv_b, rg_w_a, rg_b_a, rg_w_x, rg_b_x, rg_lam, rg_w_out,
           cm_w_in, cm_b_in, cm_ln_g, cm_ln_b, cm_w_s, cm_b_s, cm_w_out,
           ffn_w_up, ffn_conv_w, ffn_conv_b, ffn_w_down, final_g):
    h_finals = []
    for l in range(DEPTH):
        sh1, sc1, g1, sh2, sc2, g2 = _modulation(cond, w_ada[l], b_ada[l])
        xn = _rmsnorm(x, norm1_g[l]) * (1 + sc1) + sh1
        j = l // N_MIXERS
        if l % N_MIXERS == 0:
            mix, h_fin = _rglru_mixer(xn, h0[:, j], rg_w_in[j], rg_conv_w[j], rg_conv_b[j],
                                      rg_w_a[j], rg_b_a[j], rg_w_x[j], rg_b_x[j], rg_lam[j],
                                      rg_w_out[j])
            h_finals.append(h_fin)
        else:
            mix = _chunk_gmlp_mixer(xn, cm_w_in[j], cm_b_in[j], cm_ln_g[j], cm_ln_b[j],
                                    cm_w_s[j], cm_b_s[j], cm_w_out[j])
        x = x + g1 * mix
        xn = _rmsnorm(x, norm2_g[l]) * (1 + sc2) + sh2
        x = x + g2 * _conv_ffn(xn, ffn_w_up[l], ffn_conv_w[l], ffn_conv_b[l], ffn_w_down[l])
    return _rmsnorm(x, final_g), jnp.stack(h_finals, axis=1)


def setup_inputs(seed: int = 0) -> dict:
    key = jax.random.key(seed)
    ks = iter(jax.random.split(key, 40))
    nrm = lambda shape, s: jax.random.normal(next(ks), shape, jnp.float32) * s
    D = D_MODEL
    a0 = jax.random.uniform(next(ks), (N_RG, 2, D_RNN), jnp.float32, 0.9, 0.999)
    sg = a0 ** (1.0 / RG_C)
    rg_lam = jnp.log(sg) - jnp.log1p(-sg)
    return {
        'x_prompt': nrm((BATCH, SEQ, D), 1.0),
        'x_sample': nrm((DEC_BATCH, DEC_SEQ, D), 1.0),
        'state_rglru': nrm((DEC_BATCH, N_RG, 2, D_RNN), 0.5),
        'c': nrm((DEC_BATCH, D), 1.0),
        'c_ctx': nrm((D,), 1.0),
        'norm1_g': 1.0 + nrm((DEPTH, D), 0.05),
        'norm2_g': 1.0 + nrm((DEPTH, D), 0.05),
        'w_ada': nrm((DEPTH, D, 6 * D), 0.5 * D ** -0.5),
        'b_ada': nrm((DEPTH, 6 * D), 0.02),
        'rg_w_in': nrm((N_RG, D, 2 * D_RNN), D ** -0.5),
        'rg_conv_w': nrm((N_RG, RG_CONV_W, D_RNN), RG_CONV_W ** -0.5),
        'rg_conv_b': nrm((N_RG, D_RNN), 0.02),
        'rg_w_a': nrm((N_RG, 2, RG_HEADS, RG_BW, RG_BW), RG_BW ** -0.5),
        'rg_b_a': nrm((N_RG, 2, D_RNN), 0.02),
        'rg_w_x': nrm((N_RG, 2, RG_HEADS, RG_BW, RG_BW), RG_BW ** -0.5),
        'rg_b_x': nrm((N_RG, 2, D_RNN), 0.02),
        'rg_lam': rg_lam,
        'rg_w_out': nrm((N_RG, D_RNN, D), D_RNN ** -0.5),
        'cm_w_in': nrm((N_CM, D, 2 * CM_WIDTH), D ** -0.5),
        'cm_b_in': nrm((N_CM, 2 * CM_WIDTH), 0.02),
        'cm_ln_g': 1.0 + nrm((N_CM, CM_WIDTH), 0.05),
        'cm_ln_b': nrm((N_CM, CM_WIDTH), 0.02),
        'cm_w_s': nrm((N_CM, CM_GROUPS, CHUNK, CHUNK), CHUNK ** -0.5),
        'cm_b_s': 1.0 + nrm((N_CM, CM_GROUPS, CHUNK), 0.05),
        'cm_w_out': nrm((N_CM, CM_WIDTH, D), CM_WIDTH ** -0.5),
        'ffn_w_up': nrm((DEPTH, D, 2 * D_FF), D ** -0.5),
        'ffn_conv_w': nrm((DEPTH, FFN_CONV_W, 2 * D_FF), FFN_CONV_W ** -0.5),
        'ffn_conv_b': nrm((DEPTH, 2 * D_FF), 0.02),
        'ffn_w_down': nrm((DEPTH, D_FF, D), D_FF ** -0.5),
        'final_g': 1.0 + nrm((D,), 0.05),
    }


def reference(x_prompt, x_sample, state_rglru, c, c_ctx, norm1_g, norm2_g, w_ada, b_ada,
              rg_w_in, rg_conv_w, rg_conv_b, rg_w_a, rg_b_a, rg_w_x, rg_b_x, rg_lam, rg_w_out,
              cm_w_in, cm_b_in, cm_ln_g, cm_ln_b, cm_w_s, cm_b_s, cm_w_out,
              ffn_w_up, ffn_conv_w, ffn_conv_b, ffn_w_down, final_g):
    h0_ctx = jnp.zeros((x_prompt.shape[0], N_RG, 2, D_RNN), jnp.float32)
    y_prompt, new_state_rglru = _trunk(
        x_prompt, c_ctx, h0_ctx, norm1_g, norm2_g, w_ada, b_ada,
        rg_w_in, rg_conv_w, rg_conv_b, rg_w_a, rg_b_a, rg_w_x, rg_b_x, rg_lam, rg_w_out,
        cm_w_in, cm_b_in, cm_ln_g, cm_ln_b, cm_w_s, cm_b_s, cm_w_out,
        ffn_w_up, ffn_conv_w, ffn_conv_b, ffn_w_down, final_g)
    rows = x_sample.shape[1] // GRID_W
    xs = x_sample + _grid_pos_embed(rows, x_sample.dtype)
    y_sample, _ = _trunk(
        xs, c, state_rglru, norm1_g, norm2_g, w_ada, b_ada,
        rg_w_in, rg_conv_w, rg_conv_b, rg_w_a, rg_b_a, rg_w_x, rg_b_x, rg_lam, rg_w_out,
        cm_w_in, cm_b_in, cm_ln_g, cm_ln_b, cm_w_s, cm_b_s, cm_w_out,
        ffn_w_up, ffn_conv_w, ffn_conv_b, ffn_w_down, final_g)
    return (y_prompt, y_sample, new_state_rglru)
```

```python
import functools
import math

import jax
import jax.numpy as jnp
from jax import lax
from jax.experimental import pallas as pl
from jax.experimental.pallas import tpu as pltpu

F32 = jnp.float32
BF16 = jnp.bfloat16

GRID_W = 64
RG_C = 8.0
RG_BW = 128
CHUNK = 128
CM_GROUPS = 8
EPS = 1e-6
LN_EPS = 1e-5
SUBLANES = 8
VMEM_LIMIT = 60 * 1024 * 1024


def _params(n_parallel, n_arbitrary):
    sem = ("parallel",) * n_parallel + ("arbitrary",) * n_arbitrary
    return pltpu.CompilerParams(dimension_semantics=sem, vmem_limit_bytes=VMEM_LIMIT)


def _rms_mod(x, g, scale, shift):
    ms = jnp.mean(x * x, axis=-1, keepdims=True)
    return x * lax.rsqrt(ms + EPS) * g * (1.0 + scale) + shift


def _shift_rows(x, k, tpos, seq_len):
    rows = x.shape[0]
    y = pltpu.roll(x, (-k) % rows, axis=0)
    valid = (tpos + k >= 0) & (tpos + k < seq_len)
    return jnp.where(valid, y, 0.0)


def _row_pos(shape, seq_len):
    return lax.broadcasted_iota(jnp.int32, shape, 0) & (seq_len - 1)


def _mod_kernel(c_ref, w_ref, b_ref, o_ref):
    s = jax.nn.silu(c_ref[...]).astype(BF16)
    o_ref[0] = jnp.dot(s, w_ref[0].astype(BF16), preferred_element_type=F32) + b_ref[0]


def _modulation(cond, w_ada, b_ada, ns=1024):
    depth, d, n = w_ada.shape
    m = cond.shape[0]
    return pl.pallas_call(
        _mod_kernel,
        grid=(depth, n // ns),
        in_specs=[
            pl.BlockSpec((m, d), lambda l, j: (0, 0)),
            pl.BlockSpec((1, d, ns), lambda l, j: (l, 0, j)),
            pl.BlockSpec((1, 1, ns), lambda l, j: (l, 0, j)),
        ],
        out_specs=pl.BlockSpec((1, m, ns), lambda l, j: (l, 0, j)),
        out_shape=jax.ShapeDtypeStruct((depth, m, n), F32),
        compiler_params=_params(2, 0),
        name="modulation",
    )(cond, w_ada, b_ada.reshape(depth, 1, n))


def _pos_kernel(o_ref):
    rows, cols, d = o_ref.shape
    nf = d // 4
    k = lax.broadcasted_iota(jnp.int32, (1, nf), 1).astype(F32)
    omega = 1.0 / jnp.exp((k / nf) * math.log(10000.0))
    rr = lax.broadcasted_iota(jnp.int32, (rows, 1), 0).astype(F32)
    cc = lax.broadcasted_iota(jnp.int32, (cols, 1), 0).astype(F32)
    ar = rr * omega
    ac = cc * omega
    o_ref[:, :, 0 * nf:1 * nf] = jnp.broadcast_to(jnp.sin(ar)[:, None, :], (rows, cols, nf))
    o_ref[:, :, 1 * nf:2 * nf] = jnp.broadcast_to(jnp.cos(ar)[:, None, :], (rows, cols, nf))
    o_ref[:, :, 2 * nf:3 * nf] = jnp.broadcast_to(jnp.sin(ac)[None, :, :], (rows, cols, nf))
    o_ref[:, :, 3 * nf:4 * nf] = jnp.broadcast_to(jnp.cos(ac)[None, :, :], (rows, cols, nf))


def _grid_pos_table(seq_len, d):
    rows = seq_len // GRID_W
    tab = pl.pallas_call(
        _pos_kernel,
        out_shape=jax.ShapeDtypeStruct((rows, GRID_W, d), F32),
        name="grid_pos",
    )()
    return tab.reshape(seq_len, d)


def _in_proj_kernel(*refs, n_act, add_pos):
    if add_pos:
        x_ref, pos_ref, mod_ref, g_ref, w_ref, h_ref, x0_ref, xn_scr = refs
    else:
        x_ref, mod_ref, g_ref, w_ref, h_ref, xn_scr = refs
    j = pl.program_id(1)

    @pl.when(j == 0)
    def _():
        x = x_ref[0]
        if add_pos:
            x = x + pos_ref[...]
            x0_ref[0] = x
        xn = _rms_mod(x, g_ref[...], mod_ref[0, 1:2, :], mod_ref[0, 0:1, :])
        xn_scr[...] = xn.astype(BF16)

    h = jnp.dot(xn_scr[...], w_ref[...], preferred_element_type=F32)

    @pl.when(j < n_act)
    def _():
        h_ref[0] = jax.nn.gelu(h, approximate=True)

    @pl.when(j >= n_act)
    def _():
        h_ref[0] = h


def _in_proj(x, pos, mod, g, w, *, seq_len, per_seq_mod, r=512, ns=1024):
    ntok, d = x.shape
    n = w.shape[1]
    nt, nj = ntok // r, n // ns
    tiles_per_mod = (seq_len // r) if per_seq_mod else nt
    add_pos = pos is not None
    x3 = x.reshape(nt, r, d)
    in_specs = [pl.BlockSpec((1, r, d), lambda i, j: (i, 0, 0))]
    args = [x3]
    if add_pos:
        tiles_per_seq = seq_len // r
        in_specs.append(pl.BlockSpec((r, d), lambda i, j: (i % tiles_per_seq, 0)))
        args.append(pos)
    in_specs += [
        pl.BlockSpec((1, 6, d), lambda i, j: (i // tiles_per_mod, 0, 0)),
        pl.BlockSpec((1, d), lambda i, j: (0, 0)),
        pl.BlockSpec((d, ns), lambda i, j: (0, j)),
    ]
    args += [mod, g.reshape(1, d), w]
    out_specs = [pl.BlockSpec((1, r, ns), lambda i, j: (i, 0, j))]
    out_shape = [jax.ShapeDtypeStruct((nt, r, n), F32)]
    if add_pos:
        out_specs.append(pl.BlockSpec((1, r, d), lambda i, j: (i, 0, 0)))
        out_shape.append(jax.ShapeDtypeStruct((nt, r, d), F32))
    outs = pl.pallas_call(
        functools.partial(_in_proj_kernel, n_act=nj // 2, add_pos=add_pos),
        grid=(nt, nj),
        in_specs=in_specs,
        out_specs=out_specs,
        out_shape=out_shape,
        scratch_shapes=[pltpu.VMEM((r, d), BF16)],
        compiler_params=_params(1, 1),
        name="rg_in_proj",
    )(*args)
    h = outs[0].reshape(ntok, n)
    return (h, outs[1].reshape(ntok, d)) if add_pos else (h, x)


def _tile_scan(a, u, reverse):
    rows, width = a.shape
    a = a.reshape(rows // SUBLANES, SUBLANES, width)
    u = u.reshape(rows // SUBLANES, SUBLANES, width)
    sub = lax.broadcasted_iota(jnp.int32, a.shape, 1)
    step = 1
    while step < SUBLANES:
        if reverse:
            a_s = pltpu.roll(a, SUBLANES - step, axis=1)
            u_s = pltpu.roll(u, SUBLANES - step, axis=1)
            valid = sub < SUBLANES - step
        else:
            a_s = pltpu.roll(a, step, axis=1)
            u_s = pltpu.roll(u, step, axis=1)
            valid = sub >= step
        u = u + a * jnp.where(valid, u_s, 0.0)
        a = a * jnp.where(valid, a_s, 1.0)
        step *= 2
    return a.reshape(rows, width), u.reshape(rows, width)


def _rglru_kernel(*refs, bb, seq_len, heads, has_h0):
    if has_h0:
        (gate_ref, xb_ref, cw_ref, cb_ref, wcat_ref, bcat_ref, lam_ref, h0_ref,
         yg_ref, hfin_ref, a0_scr, u0_scr, a1_scr, u1_scr) = refs
    else:
        (gate_ref, xb_ref, cw_ref, cb_ref, wcat_ref, bcat_ref, lam_ref,
         yg_ref, hfin_ref, a0_scr, u0_scr, a1_scr, u1_scr) = refs
    rows = bb * seq_len
    cs = heads * RG_BW
    tpos = _row_pos((rows, RG_BW), seq_len)
    neg_lam = -lam_ref[...]
    softplus = jnp.maximum(neg_lam, 0.0) + jnp.log1p(jnp.exp(-jnp.abs(neg_lam)))

    for h in range(heads):
        sl = slice(h * RG_BW, (h + 1) * RG_BW)
        x = xb_ref[:, :, sl].reshape(rows, RG_BW)
        xb = (cw_ref[0:1, sl] * _shift_rows(x, -2, tpos, seq_len)
              + cw_ref[1:2, sl] * _shift_rows(x, -1, tpos, seq_len)
              + cw_ref[2:3, sl] * x
              + cw_ref[3:4, sl] * _shift_rows(x, 1, tpos, seq_len)
              + cb_ref[:, sl])
        z = jnp.dot(xb.astype(BF16), wcat_ref[h], preferred_element_type=F32) + bcat_ref[h]
        for d, (a_scr, u_scr) in enumerate(((a0_scr, u0_scr), (a1_scr, u1_scr))):
            r = jax.nn.sigmoid(z[:, d * RG_BW:(d + 1) * RG_BW])
            ig = jax.nn.sigmoid(z[:, (2 + d) * RG_BW:(3 + d) * RG_BW])
            log_a = -RG_C * r * softplus[d:d + 1, sl]
            a = jnp.exp(log_a)
            t = jnp.tanh(log_a)
            u = jnp.sqrt(-2.0 * t / (1.0 - t)) * (ig * xb)
            a, u = _tile_scan(a, u, reverse=(d == 1))
            a_scr[:, sl] = a
            u_scr[:, sl] = u

    def body(k, carry):
        out = []
        for s in range(bb):
            cf, cb = carry[s]
            rf = pl.multiple_of(s * seq_len + k * SUBLANES, SUBLANES)
            rb = pl.multiple_of(s * seq_len + seq_len - SUBLANES - k * SUBLANES, SUBLANES)
            hf = a0_scr[pl.ds(rf, SUBLANES), :] * cf + u0_scr[pl.ds(rf, SUBLANES), :]
            u0_scr[pl.ds(rf, SUBLANES), :] = hf
            hb = a1_scr[pl.ds(rb, SUBLANES), :] * cb + u1_scr[pl.ds(rb, SUBLANES), :]
            u1_scr[pl.ds(rb, SUBLANES), :] = hb
            out.append((jnp.broadcast_to(hf[SUBLANES - 1:SUBLANES, :], (SUBLANES, cs)),
                        jnp.broadcast_to(hb[0:1, :], (SUBLANES, cs))))
        return tuple(out)

    init = []
    for s in range(bb):
        if has_h0:
            init.append((jnp.broadcast_to(h0_ref[s, 0:1, :], (SUBLANES, cs)),
                         jnp.broadcast_to(h0_ref[s, 1:2, :], (SUBLANES, cs))))
        else:
            init.append((jnp.zeros((SUBLANES, cs), F32), jnp.zeros((SUBLANES, cs), F32)))
    final = lax.fori_loop(0, seq_len // SUBLANES, body, tuple(init))
    for s in range(bb):
        hfin_ref[s, 0:1, :] = final[s][0][0:1, :]
        hfin_ref[s, 1:2, :] = final[s][1][0:1, :]

    y = u0_scr[...] + u1_scr[...]
    yg_ref[...] = (y * gate_ref[...].reshape(rows, cs)).astype(BF16).reshape(bb, seq_len, cs)


def _rglru(h, h0, conv_w, conv_b, wcat, bcat, lam, *, batch, seq_len, bb, heads=4):
    d = h.shape[-1] // 2
    cs = heads * RG_BW
    ns = d // cs
    has_h0 = h0 is not None
    in_specs = [
        pl.BlockSpec((bb, seq_len, cs), lambda i, j: (i, 0, j)),
        pl.BlockSpec((bb, seq_len, cs), lambda i, j: (i, 0, ns + j)),
        pl.BlockSpec((4, cs), lambda i, j: (0, j)),
        pl.BlockSpec((1, cs), lambda i, j: (0, j)),
        pl.BlockSpec((heads, RG_BW, 4 * RG_BW), lambda i, j: (j, 0, 0)),
        pl.BlockSpec((heads, 1, 4 * RG_BW), lambda i, j: (j, 0, 0)),
        pl.BlockSpec((2, cs), lambda i, j: (0, j)),
    ]
    args = [h, h, conv_w, conv_b.reshape(1, d), wcat, bcat, lam]
    if has_h0:
        in_specs.append(pl.BlockSpec((bb, 2, cs), lambda i, j: (i, 0, j)))
        args.append(h0)
    return pl.pallas_call(
        functools.partial(_rglru_kernel, bb=bb, seq_len=seq_len, heads=heads, has_h0=has_h0),
        grid=(batch // bb, ns),
        in_specs=in_specs,
        out_specs=[
            pl.BlockSpec((bb, seq_len, cs), lambda i, j: (i, 0, j)),
            pl.BlockSpec((bb, 2, cs), lambda i, j: (i, 0, j)),
        ],
        out_shape=[
            jax.ShapeDtypeStruct((batch, seq_len, d), BF16),
            jax.ShapeDtypeStruct((batch, 2, d), F32),
        ],
        scratch_shapes=[pltpu.VMEM((bb * seq_len, cs), F32)] * 4,
        compiler_params=_params(2, 0),
        name="rglru_scan",
    )(*args)


def _out_proj_kernel(a_ref, w_ref, x_ref, mod_ref, o_ref, *, gate_row):
    acc = jnp.dot(a_ref[0], w_ref[...], preferred_element_type=F32)
    o_ref[0] = x_ref[0] + mod_ref[0, gate_row:gate_row + 1, :] * acc


def _out_proj(a, w, x, mod, *, seq_len, per_seq_mod, gate_row, r=1024, ns=1024):
    ntok, k = a.shape
    d = w.shape[1]
    nt, nj = ntok // r, d // ns
    tiles_per_mod = (seq_len // r) if per_seq_mod else nt
    out = pl.pallas_call(
        functools.partial(_out_proj_kernel, gate_row=gate_row),
        grid=(nt, nj),
        in_specs=[
            pl.BlockSpec((1, r, k), lambda i, j: (i, 0, 0)),
            pl.BlockSpec((k, ns), lambda i, j: (0, j)),
            pl.BlockSpec((1, r, ns), lambda i, j: (i, 0, j)),
            pl.BlockSpec((1, 6, ns), lambda i, j: (i // tiles_per_mod, 0, j)),
        ],
        out_specs=pl.BlockSpec((1, r, ns), lambda i, j: (i, 0, j)),
        out_shape=jax.ShapeDtypeStruct((nt, r, d), F32),
        compiler_params=_params(2, 0),
        name="out_proj",
    )(a.reshape(nt, r, k), w, x.reshape(nt, r, d), mod)
    return out.reshape(ntok, d)


def _ffn_kernel(*refs, seq_len, nf, n_split, final_norm):
    if final_norm:
        (x_ref, mod_ref, g_ref, wg_ref, wv_ref, cwg_ref, cwv_ref, cbg_ref, cbv_ref, wd_ref, fg_ref,
         o_ref, xn_scr) = refs
    else:
        (x_ref, mod_ref, g_ref, wg_ref, wv_ref, cwg_ref, cwv_ref, cbg_ref, cbv_ref, wd_ref,
         o_ref, xn_scr) = refs
    j = pl.program_id(1)

    @pl.when(j == 0)
    def _():
        x = x_ref[0]
        xn_scr[...] = _rms_mod(x, g_ref[...], mod_ref[0, 4:5, :], mod_ref[0, 3:4, :]).astype(BF16)
        o_ref[0] = x

    xn = xn_scr[...]
    rows = xn.shape[0]
    fs = wg_ref.shape[1]
    tpos = _row_pos((rows, fs), seq_len)

    def conv3(hh, cw_ref, cb_ref):
        return (cw_ref[0:1, :] * _shift_rows(hh, -1, tpos, seq_len) + cw_ref[1:2, :] * hh
                + cw_ref[2:3, :] * _shift_rows(hh, 1, tpos, seq_len) + cb_ref[...])

    g = conv3(jnp.dot(xn, wg_ref[...], preferred_element_type=F32), cwg_ref, cbg_ref)
    v = conv3(jnp.dot(xn, wv_ref[...], preferred_element_type=F32), cwv_ref, cbv_ref)
    act = (jax.nn.silu(g) * v).astype(BF16)
    dn = o_ref.shape[2] // n_split
    for n in range(n_split):
        cols = slice(n * dn, (n + 1) * dn)
        o_ref[0, :, cols] += mod_ref[0, 5:6, cols] * jnp.dot(act, wd_ref[:, cols], preferred_element_type=F32)

    if final_norm:
        @pl.when(j == nf - 1)
        def _():
            y = o_ref[0]
            ms = jnp.mean(y * y, axis=-1, keepdims=True)
            o_ref[0] = y * lax.rsqrt(ms + EPS) * fg_ref[...]


def _conv_ffn(x, mod, g, w_up, conv_w, conv_b, w_down, final_g, *, seq_len, per_seq_mod, r=1024, fs=512):
    ntok, d = x.shape
    f = w_down.shape[0]
    nt, nf = ntok // r, f // fs
    tiles_per_mod = (seq_len // r) if per_seq_mod else nt
    final_norm = final_g is not None
    conv_b = conv_b.reshape(1, 2 * f)
    in_specs = [
        pl.BlockSpec((1, r, d), lambda i, j: (i, 0, 0), pipeline_mode=pl.Buffered(1)),
        pl.BlockSpec((1, 6, d), lambda i, j: (i // tiles_per_mod, 0, 0)),
        pl.BlockSpec((1, d), lambda i, j: (0, 0)),
        pl.BlockSpec((d, fs), lambda i, j: (0, j)),
        pl.BlockSpec((d, fs), lambda i, j: (0, nf + j)),
        pl.BlockSpec((3, fs), lambda i, j: (0, j)),
        pl.BlockSpec((3, fs), lambda i, j: (0, nf + j)),
        pl.BlockSpec((1, fs), lambda i, j: (0, j)),
        pl.BlockSpec((1, fs), lambda i, j: (0, nf + j)),
        pl.BlockSpec((fs, d), lambda i, j: (j, 0)),
    ]
    args = [x.reshape(nt, r, d), mod, g.reshape(1, d), w_up, w_up, conv_w, conv_w, conv_b, conv_b, w_down]
    if final_norm:
        in_specs.append(pl.BlockSpec((1, d), lambda i, j: (0, 0)))
        args.append(final_g.reshape(1, d))
    out = pl.pallas_call(
        functools.partial(_ffn_kernel, seq_len=min(seq_len, r), nf=nf, n_split=4, final_norm=final_norm),
        grid=(nt, nf),
        in_specs=in_specs,
        out_specs=pl.BlockSpec((1, r, d), lambda i, j: (i, 0, 0)),
        out_shape=jax.ShapeDtypeStruct((nt, r, d), F32),
        scratch_shapes=[pltpu.VMEM((r, d), BF16)],
        compiler_params=_params(1, 1),
        name="conv_ffn",
    )(*args)
    return out.reshape(ntok, d)


def _gmlp_kernel(x_ref, mod_ref, g_ref, w_ref, b_ref, lng_ref, lnb_ref, ws_ref, bst_ref, o_ref,
                 xn_scr, h_scr, *, nj):
    j = pl.program_id(1)

    @pl.when(j == 0)
    def _():
        xn_scr[...] = _rms_mod(x_ref[0], g_ref[...], mod_ref[0, 1:2, :], mod_ref[0, 0:1, :]).astype(BF16)

    h = jnp.dot(xn_scr[...], w_ref[...], preferred_element_type=F32) + b_ref[...]
    h_scr[j] = jax.nn.gelu(h, approximate=True)

    @pl.when(j == nj - 1)
    def _():
        half = nj // 2
        rows, ns = h_scr.shape[1], h_scr.shape[2]
        width = half * ns
        vs = [h_scr[half + p] for p in range(half)]
        mu = sum(jnp.sum(v, axis=-1, keepdims=True) for v in vs) / width
        var = sum(jnp.sum(jnp.square(v - mu), axis=-1, keepdims=True) for v in vs) / width
        inv = lax.rsqrt(var + LN_EPS)
        gw = width // CM_GROUPS
        for p in range(half):
            cols = slice(p * ns, (p + 1) * ns)
            vn = ((vs[p] - mu) * inv * lng_ref[:, cols] + lnb_ref[:, cols]).astype(BF16)
            for gl in range(ns // gw):
                grp = p * (ns // gw) + gl
                gcols = slice(gl * gw, (gl + 1) * gw)
                for n in range(rows // CHUNK):
                    rws = slice(n * CHUNK, (n + 1) * CHUNK)
                    s = (jnp.dot(ws_ref[grp], vn[rws, gcols], preferred_element_type=F32)
                         + bst_ref[:, grp:grp + 1])
                    o_ref[0, rws, p * ns + gl * gw:p * ns + (gl + 1) * gw] = (
                        h_scr[p, rws, gcols] * s).astype(BF16)


def _gmlp(x, mod, g, w_in, b_in, ln_g, ln_b, w_s, b_s_t, *, seq_len, per_seq_mod, r=512, ns=1024):
    ntok, d = x.shape
    n = w_in.shape[1]
    width = n // 2
    nt, nj = ntok // r, n // ns
    tiles_per_mod = (seq_len // r) if per_seq_mod else nt
    out = pl.pallas_call(
        functools.partial(_gmlp_kernel, nj=nj),
        grid=(nt, nj),
        in_specs=[
            pl.BlockSpec((1, r, d), lambda i, j: (i, 0, 0)),
            pl.BlockSpec((1, 6, d), lambda i, j: (i // tiles_per_mod, 0, 0)),
            pl.BlockSpec((1, d), lambda i, j: (0, 0)),
            pl.BlockSpec((d, ns), lambda i, j: (0, j)),
            pl.BlockSpec((1, ns), lambda i, j: (0, j)),
            pl.BlockSpec((1, width), lambda i, j: (0, 0)),
            pl.BlockSpec((1, width), lambda i, j: (0, 0)),
            pl.BlockSpec((CM_GROUPS, CHUNK, CHUNK), lambda i, j: (0, 0, 0)),
            pl.BlockSpec((CHUNK, CM_GROUPS), lambda i, j: (0, 0)),
        ],
        out_specs=pl.BlockSpec((1, r, width), lambda i, j: (i, 0, 0)),
        out_shape=jax.ShapeDtypeStruct((nt, r, width), BF16),
        scratch_shapes=[pltpu.VMEM((r, d), BF16), pltpu.VMEM((nj, r, ns), F32)],
        compiler_params=_params(1, 1),
        name="gmlp_mix",
    )(x.reshape(nt, r, d), mod, g.reshape(1, d), w_in, b_in.reshape(1, n), ln_g.reshape(1, width),
      ln_b.reshape(1, width), w_s, b_s_t)
    return out.reshape(ntok, width)


def _trunk(x, pos, mods, h0, p, *, batch, seq_len, per_seq_mod, rg_bb):
    d = x.shape[-1]
    kw = dict(seq_len=seq_len, per_seq_mod=per_seq_mod)
    h, x = _in_proj(x, pos, mods[0], p["norm1_g"][0], p["rg_w_in"], **kw)
    yg, h_fin = _rglru(h.reshape(batch, seq_len, 2 * d), h0, p["rg_conv_w"], p["rg_conv_b"], p["rg_wcat"],
                       p["rg_bcat"], p["rg_lam"], batch=batch, seq_len=seq_len, bb=rg_bb)
    x = _out_proj(yg.reshape(batch * seq_len, d), p["rg_w_out"], x, mods[0], gate_row=2, **kw)
    x = _conv_ffn(x, mods[0], p["norm2_g"][0], p["ffn_w_up"][0], p["ffn_conv_w"][0], p["ffn_conv_b"][0],
                  p["ffn_w_down"][0], None, **kw)
    us = _gmlp(x, mods[1], p["norm1_g"][1], p["cm_w_in"], p["cm_b_in"], p["cm_ln_g"], p["cm_ln_b"],
               p["cm_w_s"], p["cm_b_s_t"], **kw)
    x = _out_proj(us, p["cm_w_out"], x, mods[1], gate_row=2, **kw)
    x = _conv_ffn(x, mods[1], p["norm2_g"][1], p["ffn_w_up"][1], p["ffn_conv_w"][1], p["ffn_conv_b"][1],
                  p["ffn_w_down"][1], p["final_g"], **kw)
    return x, h_fin


def kernel(x_prompt, x_sample, state_rglru, c, c_ctx, norm1_g, norm2_g, w_ada, b_ada, rg_w_in, rg_conv_w, rg_conv_b, rg_w_a, rg_b_a, rg_w_x, rg_b_x, rg_lam, rg_w_out, cm_w_in, cm_b_in, cm_ln_g, cm_ln_b, cm_w_s, cm_b_s, cm_w_out, ffn_w_up, ffn_conv_w, ffn_conv_b, ffn_w_down, final_g):
    batch, seq, d = x_prompt.shape
    dec_batch, dec_seq, _ = x_sample.shape
    depth = w_ada.shape[0]
    assert depth == 2 and rg_w_in.shape[0] == 1 and cm_w_in.shape[0] == 1
    heads = d // RG_BW

    n_cond = 1 + dec_batch
    pad = (-n_cond) % SUBLANES
    cond = jnp.concatenate([c_ctx[None, :], c, jnp.zeros((pad, d), F32)], axis=0)
    mod = _modulation(cond, w_ada, b_ada).reshape(depth, n_cond + pad, 6, d)
    mods_ctx = [mod[l, 0:1] for l in range(depth)]
    mods_smp = [mod[l, 1:n_cond] for l in range(depth)]

    wcat = jnp.concatenate([rg_w_a[0, 0], rg_w_a[0, 1], rg_w_x[0, 0], rg_w_x[0, 1]], axis=-1).astype(BF16)
    bcat = jnp.concatenate([rg_b_a[0].reshape(2, heads, 1, RG_BW)[0], rg_b_a[0].reshape(2, heads, 1, RG_BW)[1],
                            rg_b_x[0].reshape(2, heads, 1, RG_BW)[0], rg_b_x[0].reshape(2, heads, 1, RG_BW)[1]],
                           axis=-1)
    p = dict(
        norm1_g=norm1_g, norm2_g=norm2_g, final_g=final_g,
        rg_w_in=rg_w_in[0].astype(BF16), rg_conv_w=rg_conv_w[0], rg_conv_b=rg_conv_b[0],
        rg_wcat=wcat, rg_bcat=bcat, rg_lam=rg_lam[0], rg_w_out=rg_w_out[0].astype(BF16),
        cm_w_in=cm_w_in[0].astype(BF16), cm_b_in=cm_b_in[0], cm_ln_g=cm_ln_g[0], cm_ln_b=cm_ln_b[0],
        cm_w_s=cm_w_s[0].astype(BF16), cm_b_s_t=cm_b_s[0].T, cm_w_out=cm_w_out[0].astype(BF16),
        ffn_w_up=ffn_w_up.astype(BF16), ffn_conv_w=ffn_conv_w, ffn_conv_b=ffn_conv_b,
        ffn_w_down=ffn_w_down.astype(BF16),
    )

    y_prompt, h_fin = _trunk(x_prompt.reshape(batch * seq, d), None, mods_ctx, None, p,
                             batch=batch, seq_len=seq, per_seq_mod=False, rg_bb=4)
    pos = _grid_pos_table(dec_seq, d)
    y_sample, _ = _trunk(x_sample.reshape(dec_batch * dec_seq, d), pos, mods_smp, state_rglru[:, 0], p,
                         batch=dec_batch, seq_len=dec_seq, per_seq_mod=True, rg_bb=1)
    new_state = h_fin.reshape(batch, 1, 2, d)
    return (y_prompt.reshape(batch, seq, d), y_sample.reshape(dec_batch, dec_seq, d), new_state)
```
